```python
import jax, jax.numpy as jnp
from jax import lax
import numpy as np

D_MODEL = 1024
BATCH = 8
SEQ = 2048
DEPTH = 4
DEC_BATCH = 128
DEC_SEQ = 1
PAST_LEN = 16384
PAGE_SIZE = 128

N_MIXERS = 2
N_POOL_LAYERS = (DEPTH + 1) // 2
N_GMLP_LAYERS = DEPTH // 2
POOL_WINDOWS = (2, 4, 8, 16)
POOL_GROUP = D_MODEL // len(POOL_WINDOWS)
POOL_BUF = max(POOL_WINDOWS) - 1
CHUNK = 128
GMLP_GROUPS = 8
GMLP_WIDTH = D_MODEL
GMLP_GC = GMLP_WIDTH // GMLP_GROUPS
D_FF = ((8 * D_MODEL // 3 + 255) // 256) * 256
EPS = 1e-6

kernel_name = "hybrid_pool_gmlp_adaln_decoder_step"


def _rmsnorm(x, g):
    xf = x.astype(jnp.float32)
    y = xf * lax.rsqrt(jnp.mean(xf * xf, axis=-1, keepdims=True) + EPS)
    return (y * g.astype(jnp.float32)).astype(x.dtype)


def _layernorm(x, g, b):
    xf = x.astype(jnp.float32)
    mu = jnp.mean(xf, axis=-1, keepdims=True)
    var = jnp.mean(jnp.square(xf - mu), axis=-1, keepdims=True)
    y = (xf - mu) * lax.rsqrt(var + EPS)
    return (y * g.astype(jnp.float32) + b.astype(jnp.float32)).astype(x.dtype)


def _adaln(c, w, b):
    m = jax.nn.silu(c) @ w + b
    return jnp.split(m[:, None, :], 6, axis=-1)


def _pool_mixer(h_ext, valid_ext, w_grp, scale, w_out):
    B, L, D = h_ext.shape
    T = L - POOL_BUF
    hf = h_ext.astype(jnp.float32)
    cs = jnp.pad(jnp.cumsum(hf, axis=1), ((0, 0), (1, 0), (0, 0)))
    cv = jnp.pad(jnp.cumsum(valid_ext.astype(jnp.float32), axis=1), ((0, 0), (1, 0)))
    groups = []
    for g, w in enumerate(POOL_WINDOWS):
        lo, hi = g * POOL_GROUP, (g + 1) * POOL_GROUP
        s = cs[:, POOL_BUF + 1:, lo:hi] - cs[:, POOL_BUF + 1 - w:L + 1 - w, lo:hi]
        n = cv[:, POOL_BUF + 1:] - cv[:, POOL_BUF + 1 - w:L + 1 - w]
        groups.append(s / n[..., None])
    pooled = jnp.concatenate(groups, axis=-1) - hf[:, POOL_BUF:]
    pooled = pooled.reshape(B, T, len(POOL_WINDOWS), POOL_GROUP)
    mixed = jnp.einsum('btgc,gcd->btgd', pooled, w_grp.astype(jnp.float32))
    mixed = mixed.reshape(B, T, D) * scale.astype(jnp.float32)
    return mixed.astype(h_ext.dtype) @ w_out


def _gmlp_mixer(h, w_in, ln_g, ln_b, w_s, b_s, w_out):
    B, T, D = h.shape
    uv = h @ w_in
    u, v = uv[..., :GMLP_WIDTH], uv[..., GMLP_WIDTH:]
    v = _layernorm(v, ln_g, ln_b)
    Lc = CHUNK if T >= CHUNK else T
    n_chunks = T // Lc
    vc = v.reshape(B, n_chunks, Lc, GMLP_GROUPS, GMLP_GC)
    ws = jnp.tril(w_s[:, :Lc, :Lc])
    mixed = jnp.einsum('gts,bcsgd->bctgd', ws, vc) + b_s[:, :Lc].T[None, None, :, :, None]
    out = u * mixed.reshape(B, T, GMLP_WIDTH)
    return out @ w_out, v


def _run_group(x, c, pool_hist, pool_valid,
               w_ada, b_ada, norm_mix, norm_ffn, norm_final,
               pool_w_grp, pool_scale, pool_w_out,
               gmlp_w_in, gmlp_ln_g, gmlp_ln_b, gmlp_w_s, gmlp_b_s, gmlp_w_out,
               ffn_w_gate, ffn_w_up, ffn_w_down):
    new_pool, new_v = [], []
    for i in range(DEPTH):
        sh1, sc1, g1, sh2, sc2, g2 = _adaln(c, w_ada[i], b_ada[i])
        h = _rmsnorm(x, norm_mix[i]) * (1 + sc1) + sh1
        if i % N_MIXERS == 0:
            j = i // N_MIXERS
            h_ext = jnp.concatenate([pool_hist[j].astype(h.dtype), h], axis=1)
            out = _pool_mixer(h_ext, pool_valid, pool_w_grp[j], pool_scale[j], pool_w_out[j])
            new_pool.append(h_ext[:, -POOL_BUF:])
        else:
            j = i // N_MIXERS
            out, v = _gmlp_mixer(h, gmlp_w_in[j], gmlp_ln_g[j], gmlp_ln_b[j],
                                 gmlp_w_s[j], gmlp_b_s[j], gmlp_w_out[j])
            new_v.append(v)
        x = x + g1 * out
        h = _rmsnorm(x, norm_ffn[i]) * (1 + sc2) + sh2
        ff = (jax.nn.silu(h @ ffn_w_gate[i]) * (h @ ffn_w_up[i])) @ ffn_w_down[i]
        x = x + g2 * ff
    return _rmsnorm(x, norm_final), jnp.stack(new_pool), jnp.stack(new_v)


def setup_inputs(seed: int = 0) -> dict:
    key = jax.random.key(seed)
    ks = jax.random.split(key, 24)
    D, F = D_MODEL, D_FF
    nrm = lambda k, shape, s: jax.random.normal(k, shape, jnp.float32) * s
    return {
        "x_prompt": nrm(ks[0], (BATCH, SEQ, D), 1.0),
        "x_sample": nrm(ks[1], (DEC_BATCH, DEC_SEQ, D), 1.0),
        "state_pool": nrm(ks[2], (N_POOL_LAYERS, DEC_BATCH, POOL_BUF, D), 1.0),
        "c_prompt": nrm(ks[3], (BATCH, D), 1.0),
        "c_sample": nrm(ks[4], (DEC_BATCH, D), 1.0),
        "w_ada": nrm(ks[5], (DEPTH, D, 6 * D), 0.5 * D ** -0.5),
        "b_ada": nrm(ks[6], (DEPTH, 6 * D), 0.02),
        "norm_mix": 1.0 + nrm(ks[7], (DEPTH, D), 0.05),
        "norm_ffn": 1.0 + nrm(ks[8], (DEPTH, D), 0.05),
        "norm_final": 1.0 + nrm(ks[9], (D,), 0.05),
        "pool_w_grp": nrm(ks[10], (N_POOL_LAYERS, len(POOL_WINDOWS), POOL_GROUP, POOL_GROUP), POOL_GROUP ** -0.5),
        "pool_scale": 1.0 + nrm(ks[11], (N_POOL_LAYERS, D), 0.1),
        "pool_w_out": nrm(ks[12], (N_POOL_LAYERS, D, D), D ** -0.5),
        "gmlp_w_in": nrm(ks[13], (N_GMLP_LAYERS, D, 2 * GMLP_WIDTH), D ** -0.5),
        "gmlp_ln_g": 1.0 + nrm(ks[14], (N_GMLP_LAYERS, GMLP_WIDTH), 0.05),
        "gmlp_ln_b": nrm(ks[15], (N_GMLP_LAYERS, GMLP_WIDTH), 0.02),
        "gmlp_w_s": nrm(ks[16], (N_GMLP_LAYERS, GMLP_GROUPS, CHUNK, CHUNK), CHUNK ** -0.5),
        "gmlp_b_s": 1.0 + nrm(ks[17], (N_GMLP_LAYERS, GMLP_GROUPS, CHUNK), 0.05),
        "gmlp_w_out": nrm(ks[18], (N_GMLP_LAYERS, GMLP_WIDTH, D), GMLP_WIDTH ** -0.5),
        "ffn_w_gate": nrm(ks[19], (DEPTH, D, F), D ** -0.5),
        "ffn_w_up": nrm(ks[20], (DEPTH, D, F), D ** -0.5),
        "ffn_w_down": nrm(ks[21], (DEPTH, F, D), F ** -0.5),
    }


def reference(x_prompt, x_sample, state_pool, c_prompt, c_sample,
              w_ada, b_ada, norm_mix, norm_ffn, norm_final,
              pool_w_grp, pool_scale, pool_w_out,
              gmlp_w_in, gmlp_ln_g, gmlp_ln_b, gmlp_w_s, gmlp_b_s, gmlp_w_out,
              ffn_w_gate, ffn_w_up, ffn_w_down):
    weights = (w_ada, b_ada, norm_mix, norm_ffn, norm_final,
               pool_w_grp, pool_scale, pool_w_out,
               gmlp_w_in, gmlp_ln_g, gmlp_ln_b, gmlp_w_s, gmlp_b_s, gmlp_w_out,
               ffn_w_gate, ffn_w_up, ffn_w_down)
    B, S, D = x_prompt.shape
    hist_p = jnp.zeros((N_POOL_LAYERS, B, POOL_BUF, D), x_prompt.dtype)
    valid_p = jnp.concatenate([jnp.zeros((1, POOL_BUF), jnp.float32),
                               jnp.ones((1, S), jnp.float32)], axis=1)
    y_prompt, new_pool_prompt, _ = _run_group(x_prompt, c_prompt, hist_p, valid_p, *weights)
    T = x_sample.shape[1]
    valid_s = jnp.ones((1, POOL_BUF + T), jnp.float32)
    y_sample, new_pool_sample, new_chunk_v_sample = _run_group(
        x_sample, c_sample, state_pool, valid_s, *weights)
    return (y_prompt, y_sample, new_pool_prompt, new_pool_sample, new_chunk_v_sample)
```

```python
import functools

import jax
import jax.numpy as jnp
from jax import lax
from jax.experimental import pallas as pl
from jax.experimental.pallas import tpu as pltpu

F32 = jnp.float32
BF16 = jnp.bfloat16

EPS = 1e-6
POOL_WINDOWS = (2, 4, 8, 16)
POOL_BUF = max(POOL_WINDOWS) - 1
HIST_ROWS = 16
CHUNK = 128
GMLP_GROUPS = 8

V7X_LANES = 128
V7X_MXU_DIM = 256
V7X_VMEM_LIMIT_BYTES = 56 * 1024 * 1024

TOKEN_TILE = 512
FFN_CHUNK = V7X_MXU_DIM
ADA_ROWS_ALIGN = 16


def _rms_mod(x, gain, scale, shift):
    ms = jnp.mean(x * x, axis=-1, keepdims=True)
    return x * lax.rsqrt(ms + EPS) * (gain * (1.0 + scale)) + shift


def _rms(x, gain):
    ms = jnp.mean(x * x, axis=-1, keepdims=True)
    return x * lax.rsqrt(ms + EPS) * gain


def _silu_mul(g, u):
    return g / (1.0 + jnp.exp(-g)) * u


def _ffn(h2b, wg_ref, wu_ref, wd_ref, a_ref):
    d_ff = wg_ref.shape[1]
    for c in range(d_ff // FFN_CHUNK):
        sl = slice(c * FFN_CHUNK, (c + 1) * FFN_CHUNK)
        g = jnp.dot(h2b, wg_ref[:, sl], preferred_element_type=F32)
        u = jnp.dot(h2b, wu_ref[:, sl], preferred_element_type=F32)
        a_ref[:, sl] = _silu_mul(g, u).astype(BF16)
    return jnp.dot(a_ref[...], wd_ref[...], preferred_element_type=F32)


def _ada_kernel(c_ref, w_ref, b_ref, o_ref):
    c = c_ref[...]
    s = (c / (1.0 + jnp.exp(-c))).astype(BF16)
    o_ref[...] = jnp.dot(s, w_ref[...].astype(BF16),
                         preferred_element_type=F32) + b_ref[...]


def _ada_call(c_all, w_ada, b_ada):
    depth, d, d6 = w_ada.shape
    n_mod = d6 // d
    rows = c_all.shape[0]
    return pl.pallas_call(
        _ada_kernel,
        grid=(depth, n_mod),
        in_specs=[
            pl.BlockSpec((rows, d), lambda i, j: (0, 0)),
            pl.BlockSpec((None, d, d), lambda i, j: (i, 0, j)),
            pl.BlockSpec((None, 1, d), lambda i, j: (i, 0, j)),
        ],
        out_specs=pl.BlockSpec((None, None, rows, d), lambda i, j: (i, j, 0, 0)),
        out_shape=jax.ShapeDtypeStruct((depth, n_mod, rows, d), F32),
        compiler_params=pltpu.CompilerParams(
            dimension_semantics=("arbitrary", "arbitrary")),
        name="ada_mod",
    )(c_all, w_ada, b_ada.reshape(depth, 1, d6))


def _window_sums(ext):
    d = ext.shape[1]
    grp = d // len(POOL_WINDOWS)
    s2 = ext + pltpu.roll(ext, 1, 0)
    r4 = s2[:, grp:]
    s4 = r4 + pltpu.roll(r4, 2, 0)
    r8 = s4[:, grp:]
    s8 = r8 + pltpu.roll(r8, 4, 0)
    r16 = s8[:, grp:]
    s16 = r16 + pltpu.roll(r16, 8, 0)
    return jnp.concatenate(
        [s2[HIST_ROWS:, :grp], s4[HIST_ROWS:, :grp], s8[HIST_ROWS:, :grp],
         s16[HIST_ROWS:, :]], axis=1)


def _inv_count(t_idx, rows, window):
    pos = lax.broadcasted_iota(jnp.int32, (HIST_ROWS, V7X_LANES), 0)
    head = 1.0 / jnp.minimum(pos + 1, window).astype(F32)
    head = jnp.where(t_idx == 0, head, 1.0 / window)
    tail = jnp.full((rows - HIST_ROWS, V7X_LANES), 1.0 / window, F32)
    return jnp.concatenate([head, tail], axis=0)


def _pool_layer_kernel(final, x_ref, mod_ref, nmix_ref, nffn_ref, nfin_ref,
                       wgrp_ref, pscale_ref, wout_ref, wg_ref, wu_ref, wd_ref,
                       o_ref, newpool_ref, hist_ref, a_ref):
    t = pl.program_id(1)
    tm, d = x_ref.shape
    n_grp = len(POOL_WINDOWS)
    grp = d // n_grp

    @pl.when(t == 0)
    def _():
        hist_ref[...] = jnp.zeros_like(hist_ref)

    x = x_ref[...]
    mod = mod_ref[...]
    sh1, sc1, g1, sh2, sc2, g2 = [mod[k:k + 1, :] for k in range(6)]

    h = _rms_mod(x, nmix_ref[...], sc1, sh1)
    ext = jnp.concatenate([hist_ref[...], h], axis=0)
    sums = _window_sums(ext)
    tail = h[tm - HIST_ROWS:, :]
    hist_ref[...] = tail
    newpool_ref[...] = tail

    mixed = []
    for g, w in enumerate(POOL_WINDOWS):
        inv = _inv_count(t, tm, w)
        inv = jnp.concatenate([inv] * (grp // V7X_LANES), axis=1)
        sl = slice(g * grp, (g + 1) * grp)
        pooled = sums[:, sl] * inv - h[:, sl]
        mixed.append(jnp.dot(pooled.astype(BF16), wgrp_ref[g],
                             preferred_element_type=F32))
    mixed = jnp.concatenate(mixed, axis=1) * pscale_ref[...]
    out = jnp.dot(mixed.astype(BF16), wout_ref[...], preferred_element_type=F32)
    x1 = x + g1 * out

    h2b = _rms_mod(x1, nffn_ref[...], sc2, sh2).astype(BF16)
    ff = _ffn(h2b, wg_ref, wu_ref, wd_ref, a_ref)
    y = x1 + g2 * ff
    if final:
        y = _rms(y, nfin_ref[...])
    o_ref[...] = y


def _resident(shape):
    nd = len(shape)
    return pl.BlockSpec(shape, lambda b, t: (0,) * nd, pipeline_mode=pl.Buffered(1))


def _pool_layer_call(x, mod, nmix, nffn, nfin, wgrp, pscale, wout, wg, wu, wd, final):
    bsz, seq, d = x.shape
    tm = TOKEN_TILE
    d_ff = wg.shape[1]
    row = lambda v: v.reshape(1, d)
    return pl.pallas_call(
        functools.partial(_pool_layer_kernel, final),
        grid=(bsz, seq // tm),
        in_specs=[
            pl.BlockSpec((None, tm, d), lambda b, t: (b, t, 0)),
            pl.BlockSpec((None, 6, d), lambda b, t: (b, 0, 0)),
            _resident((1, d)), _resident((1, d)), _resident((1, d)),
            _resident(wgrp.shape), _resident((1, d)), _resident(wout.shape),
            _resident(wg.shape), _resident(wu.shape), _resident(wd.shape),
        ],
        out_specs=[
            pl.BlockSpec((None, tm, d), lambda b, t: (b, t, 0)),
            pl.BlockSpec((None, HIST_ROWS, d), lambda b, t: (b, 0, 0)),
        ],
        out_shape=[
            jax.ShapeDtypeStruct((bsz, seq, d), F32),
            jax.ShapeDtypeStruct((bsz, HIST_ROWS, d), F32),
        ],
        scratch_shapes=[
            pltpu.VMEM((HIST_ROWS, d), F32),
            pltpu.VMEM((tm, d_ff), BF16),
        ],
        compiler_params=pltpu.CompilerParams(
            dimension_semantics=("arbitrary", "arbitrary"),
            vmem_limit_bytes=V7X_VMEM_LIMIT_BYTES),
        name="prompt_pool_layer",
    )(x, mod, row(nmix), row(nffn), row(nfin), wgrp, row(pscale), wout, wg, wu, wd)


def _layernorm(v, gain, bias):
    mu = jnp.mean(v, axis=-1, keepdims=True)
    vc = v - mu
    var = jnp.mean(vc * vc, axis=-1, keepdims=True)
    return vc * lax.rsqrt(var + EPS) * gain + bias


def _gmlp_layer_kernel(final, x_ref, mod_ref, nmix_ref, nffn_ref, nfin_ref,
                       win_ref, lng_ref, lnb_ref, ws_ref, bsp_ref, wout_ref,
                       wg_ref, wu_ref, wd_ref, o_ref, um_ref, a_ref):
    tm, d = x_ref.shape
    gc = d // GMLP_GROUPS
    n_chunks = tm // CHUNK

    x = x_ref[...]
    mod = mod_ref[...]
    sh1, sc1, g1, sh2, sc2, g2 = [mod[k:k + 1, :] for k in range(6)]

    hb = _rms_mod(x, nmix_ref[...], sc1, sh1).astype(BF16)
    uv = jnp.dot(hb, win_ref[...], preferred_element_type=F32)
    u = uv[:, :d]
    vb = _layernorm(uv[:, d:], lng_ref[...], lnb_ref[...]).astype(BF16)

    row = lax.broadcasted_iota(jnp.int32, (CHUNK, CHUNK), 0)
    col = lax.broadcasted_iota(jnp.int32, (CHUNK, CHUNK), 1)
    causal = row >= col
    bsp = bsp_ref[...]
    for g in range(GMLP_GROUPS):
        lanes = slice(g * gc, (g + 1) * gc)
        ws = jnp.where(causal, ws_ref[g], 0.0).astype(BF16)
        rhs = jnp.concatenate(
            [vb[c * CHUNK:(c + 1) * CHUNK, lanes] for c in range(n_chunks)], axis=1)
        mix = jnp.dot(ws, rhs, preferred_element_type=F32)
        for c in range(n_chunks):
            rows = slice(c * CHUNK, (c + 1) * CHUNK)
            m = mix[:, c * gc:(c + 1) * gc] + bsp[:, lanes]
            um_ref[rows, lanes] = (u[rows, lanes] * m).astype(BF16)
    out = jnp.dot(um_ref[...], wout_ref[...], preferred_element_type=F32)
    x1 = x + g1 * out

    h2b = _rms_mod(x1, nffn_ref[...], sc2, sh2).astype(BF16)
    ff = _ffn(h2b, wg_ref, wu_ref, wd_ref, a_ref)
    y = x1 + g2 * ff
    if final:
        y = _rms(y, nfin_ref[...])
    o_ref[...] = y


def _gmlp_layer_call(x, mod, nmix, nffn, nfin, win, lng, lnb, ws, bsp, wout,
                     wg, wu, wd, final):
    bsz, seq, d = x.shape
    tm = TOKEN_TILE
    d_ff = wg.shape[1]
    row = lambda v: v.reshape(1, d)
    return pl.pallas_call(
        functools.partial(_gmlp_layer_kernel, final),
        grid=(bsz, seq // tm),
        in_specs=[
            pl.BlockSpec((None, tm, d), lambda b, t: (b, t, 0)),
            pl.BlockSpec((None, 6, d), lambda b, t: (b, 0, 0)),
            _resident((1, d)), _resident((1, d)), _resident((1, d)),
            _resident(win.shape), _resident((1, d)), _resident((1, d)),
            _resident(ws.shape), _resident(bsp.shape), _resident(wout.shape),
            _resident(wg.shape), _resident(wu.shape), _resident(wd.shape),
        ],
        out_specs=pl.BlockSpec((None, tm, d), lambda b, t: (b, t, 0)),
        out_shape=jax.ShapeDtypeStruct((bsz, seq, d), F32),
        scratch_shapes=[
            pltpu.VMEM((tm, d), BF16),
            pltpu.VMEM((tm, d_ff), BF16),
        ],
        compiler_params=pltpu.CompilerParams(
            dimension_semantics=("arbitrary", "arbitrary"),
            vmem_limit_bytes=V7X_VMEM_LIMIT_BYTES),
        name="prompt_gmlp_layer",
    )(x, mod, row(nmix), row(nffn), row(nfin), win, row(lng), row(lnb), ws, bsp,
      wout, wg, wu, wd)


def _sample_tail(final, c, n_c, mod_ref, nfin_ref, wg_ref, wu_ref, wd_ref,
                 o_ref, x1_ref, h2_ref, acc_ref):
    h2b = h2_ref[...]
    g = jnp.dot(h2b, wg_ref[...], preferred_element_type=F32)
    u = jnp.dot(h2b, wu_ref[...], preferred_element_type=F32)
    a = _silu_mul(g, u).astype(BF16)
    acc_ref[...] += jnp.dot(a, wd_ref[...], preferred_element_type=F32)

    @pl.when(c == n_c - 1)
    def _():
        y = x1_ref[...] + mod_ref[5] * acc_ref[...]
        if final:
            y = _rms(y, nfin_ref[...])
        o_ref[...] = y


def _sample_pool_kernel(final, x_ref, mod_ref, nmix_ref, nffn_ref, nfin_ref,
                        hist_ref, wgrp_ref, pscale_ref, wout_ref,
                        wg_ref, wu_ref, wd_ref,
                        o_ref, hnew_ref, x1_ref, h2_ref, acc_ref):
    c = pl.program_id(0)
    n_c = pl.num_programs(0)

    @pl.when(c == 0)
    def _():
        x = x_ref[...]
        d = x.shape[1]
        grp = d // len(POOL_WINDOWS)
        h = _rms_mod(x, nmix_ref[...], mod_ref[1], mod_ref[0])
        hnew_ref[...] = h
        mixed = []
        for g, w in enumerate(POOL_WINDOWS):
            sl = slice(g * grp, (g + 1) * grp)
            s = h[:, sl]
            for k in range(POOL_BUF - (w - 1), POOL_BUF):
                s = s + hist_ref[k, :, sl]
            pooled = s * (1.0 / w) - h[:, sl]
            mixed.append(jnp.dot(pooled.astype(BF16), wgrp_ref[g],
                                 preferred_element_type=F32))
        mixed = jnp.concatenate(mixed, axis=1) * pscale_ref[...]
        out = jnp.dot(mixed.astype(BF16), wout_ref[...], preferred_element_type=F32)
        x1 = x + mod_ref[2] * out
        x1_ref[...] = x1
        h2_ref[...] = _rms_mod(x1, nffn_ref[...], mod_ref[4], mod_ref[3]).astype(BF16)
        acc_ref[...] = jnp.zeros_like(acc_ref)

    _sample_tail(final, c, n_c, mod_ref, nfin_ref, wg_ref, wu_ref, wd_ref,
                 o_ref, x1_ref, h2_ref, acc_ref)


def _sample_gmlp_kernel(final, x_ref, mod_ref, nmix_ref, nffn_ref, nfin_ref,
                        win_ref, lng_ref, lnb_ref, ws0_ref, bs0_ref, wout_ref,
                        wg_ref, wu_ref, wd_ref,
                        o_ref, vnew_ref, x1_ref, h2_ref, acc_ref):
    c = pl.program_id(0)
    n_c = pl.num_programs(0)

    @pl.when(c == 0)
    def _():
        x = x_ref[...]
        d = x.shape[1]
        hb = _rms_mod(x, nmix_ref[...], mod_ref[1], mod_ref[0]).astype(BF16)
        uv = jnp.dot(hb, win_ref[...], preferred_element_type=F32)
        v = _layernorm(uv[:, d:], lng_ref[...], lnb_ref[...])
        vnew_ref[...] = v
        um = uv[:, :d] * (ws0_ref[...] * v + bs0_ref[...])
        out = jnp.dot(um.astype(BF16), wout_ref[...], preferred_element_type=F32)
        x1 = x + mod_ref[2] * out
        x1_ref[...] = x1
        h2_ref[...] = _rms_mod(x1, nffn_ref[...], mod_ref[4], mod_ref[3]).astype(BF16)
        acc_ref[...] = jnp.zeros_like(acc_ref)

    _sample_tail(final, c, n_c, mod_ref, nfin_ref, wg_ref, wu_ref, wd_ref,
                 o_ref, x1_ref, h2_ref, acc_ref)


def _fixed(shape):
    nd = len(shape)
    return pl.BlockSpec(shape, lambda c: (0,) * nd)


def _sample_call(kernel, name, x, mixer_args, wg, wu, wd, aux_shape):
    rows, d = x.shape
    d_ff = wg.shape[1]
    n_c = d_ff // FFN_CHUNK
    return pl.pallas_call(
        kernel,
        grid=(n_c,),
        in_specs=[_fixed(x.shape)] + [_fixed(a.shape) for a in mixer_args] + [
            pl.BlockSpec((d, FFN_CHUNK), lambda c: (0, c)),
            pl.BlockSpec((d, FFN_CHUNK), lambda c: (0, c)),
            pl.BlockSpec((FFN_CHUNK, d), lambda c: (c, 0)),
        ],
        out_specs=[_fixed((rows, d)), _fixed(aux_shape)],
        out_shape=[jax.ShapeDtypeStruct((rows, d), F32),
                   jax.ShapeDtypeStruct(aux_shape, F32)],
        scratch_shapes=[
            pltpu.VMEM((rows, d), F32),
            pltpu.VMEM((rows, d), BF16),
            pltpu.VMEM((rows, d), F32),
        ],
        compiler_params=pltpu.CompilerParams(
            dimension_semantics=("arbitrary",),
            vmem_limit_bytes=V7X_VMEM_LIMIT_BYTES),
        name=name,
    )(x, *mixer_args, wg, wu, wd)


def kernel(x_prompt, x_sample, state_pool, c_prompt, c_sample, w_ada, b_ada, norm_mix, norm_ffn, norm_final, pool_w_grp, pool_scale, pool_w_out, gmlp_w_in, gmlp_ln_g, gmlp_ln_b, gmlp_w_s, gmlp_b_s, gmlp_w_out, ffn_w_gate, ffn_w_up, ffn_w_down):
    depth = w_ada.shape[0]
    bsz, seq, d = x_prompt.shape
    n_s = x_sample.shape[0]
    gc = d // GMLP_GROUPS
    assert x_sample.shape[1] == 1 and seq % TOKEN_TILE == 0 and TOKEN_TILE % CHUNK == 0
    assert state_pool.shape[2] == POOL_BUF and d % (len(POOL_WINDOWS) * V7X_LANES) == 0

    n_c = bsz + n_s
    pad = (-n_c) % ADA_ROWS_ALIGN
    c_all = jnp.concatenate([c_prompt, c_sample, jnp.zeros((pad, d), F32)], axis=0)
    mods = _ada_call(c_all, w_ada, b_ada)
    mod_p = jnp.transpose(mods[:, :, :bsz], (0, 2, 1, 3))
    mod_s = mods[:, :, bsz:n_c]

    bf = lambda w: w.astype(BF16)
    wg_all, wu_all, wd_all = bf(ffn_w_gate), bf(ffn_w_up), bf(ffn_w_down)
    pool_wgrp, pool_wout = bf(pool_w_grp), bf(pool_w_out)
    gmlp_win, gmlp_wout = bf(gmlp_w_in), bf(gmlp_w_out)
    hist_t = jnp.transpose(state_pool, (0, 2, 1, 3))
    row = lambda v: v.reshape(1, d)

    xp = x_prompt
    xs = x_sample.reshape(n_s, d)
    new_pool_p, new_pool_s, new_v_s = [], [], []
    for i in range(depth):
        final = i == depth - 1
        j = i // 2
        ffn_w = (wg_all[i], wu_all[i], wd_all[i])
        norms = (norm_mix[i], norm_ffn[i], norm_final)
        if i % 2 == 0:
            xp, tail = _pool_layer_call(
                xp, mod_p[i], *norms, pool_wgrp[j], pool_scale[j], pool_wout[j],
                *ffn_w, final)
            new_pool_p.append(tail[:, HIST_ROWS - POOL_BUF:])
            xs, h_s = _sample_call(
                functools.partial(_sample_pool_kernel, final), "sample_pool_layer", xs,
                (mod_s[i], *map(row, norms), hist_t[j], pool_wgrp[j],
                 row(pool_scale[j]), pool_wout[j]),
                *ffn_w, (n_s, d))
            new_pool_s.append(
                jnp.concatenate([state_pool[j][:, 1:], h_s[:, None, :]], axis=1))
        else:
            bsp = jnp.repeat(gmlp_b_s[j].T, gc, axis=1)
            xp = _gmlp_layer_call(
                xp, mod_p[i], *norms, gmlp_win[j], gmlp_ln_g[j], gmlp_ln_b[j],
                gmlp_w_s[j], bsp, gmlp_wout[j], *ffn_w, final)
            ws0 = jnp.repeat(gmlp_w_s[j][:, 0, 0], gc).reshape(1, d)
            bs0 = jnp.repeat(gmlp_b_s[j][:, 0], gc).reshape(1, d)
            xs, v_s = _sample_call(
                functools.partial(_sample_gmlp_kernel, final), "sample_gmlp_layer", xs,
                (mod_s[i], *map(row, norms), gmlp_win[j], row(gmlp_ln_g[j]),
                 row(gmlp_ln_b[j]), ws0, bs0, gmlp_wout[j]),
                *ffn_w, (n_s, d))
            new_v_s.append(v_s[:, None, :])

    return (xp, xs.reshape(n_s, 1, d), jnp.stack(new_pool_p), jnp.stack(new_pool_s),
            jnp.stack(new_v_s))
```

```python
import functools

import jax
import jax.numpy as jnp
from jax import lax
from jax.experimental import pallas as pl
from jax.experimental.pallas import tpu as pltpu

F32 = jnp.float32
BF16 = jnp.bfloat16

EPS = 1e-6
POOL_WINDOWS = (2, 4, 8, 16)
POOL_BUF = max(POOL_WINDOWS) - 1
HIST_ROWS = 16
CHUNK = 128
GMLP_GROUPS = 8
N_MOD = 6

V7X_LANES = 128
V7X_MXU_DIM = 256
V7X_VMEM_LIMIT_BYTES = 56 * 1024 * 1024

TOKEN_TILE = 512
FFN_CHUNK = V7X_MXU_DIM
ADA_ROWS_ALIGN = 16


def _rms_mod(x, gain, scale, shift):
    ms = jnp.mean(x * x, axis=-1, keepdims=True)
    return x * lax.rsqrt(ms + EPS) * (gain * (1.0 + scale)) + shift


def _rms(x, gain):
    ms = jnp.mean(x * x, axis=-1, keepdims=True)
    return x * lax.rsqrt(ms + EPS) * gain


def _silu_mul(g, u):
    return g / (1.0 + jnp.exp(-g)) * u


def _ffn(h2b, wg_ref, wu_ref, wd_ref, a_ref):
    d_ff = wg_ref.shape[1]
    for c in range(d_ff // FFN_CHUNK):
        sl = slice(c * FFN_CHUNK, (c + 1) * FFN_CHUNK)
        g = jnp.dot(h2b, wg_ref[:, sl], preferred_element_type=F32)
        u = jnp.dot(h2b, wu_ref[:, sl], preferred_element_type=F32)
        a_ref[:, sl] = _silu_mul(g, u).astype(BF16)
    return jnp.dot(a_ref[...], wd_ref[...], preferred_element_type=F32)


def _ada_kernel(c_ref, w_ref, b_ref, o_ref):
    c = c_ref[...]
    s = (c / (1.0 + jnp.exp(-c))).astype(BF16)
    o_ref[...] = jnp.dot(s, w_ref[...].astype(BF16),
                         preferred_element_type=F32) + b_ref[...]


def _ada_call(c_all, w_ada, b_ada):
    depth, d, d6 = w_ada.shape
    n_mod = d6 // d
    rows = c_all.shape[0]
    return pl.pallas_call(
        _ada_kernel,
        grid=(depth, n_mod),
        in_specs=[
            pl.BlockSpec((rows, d), lambda i, j: (0, 0)),
            pl.BlockSpec((None, d, d), lambda i, j: (i, 0, j)),
            pl.BlockSpec((None, 1, d), lambda i, j: (i, 0, j)),
        ],
        out_specs=pl.BlockSpec((None, None, rows, d), lambda i, j: (i, j, 0, 0)),
        out_shape=jax.ShapeDtypeStruct((depth, n_mod, rows, d), F32),
        compiler_params=pltpu.CompilerParams(
            dimension_semantics=("arbitrary", "arbitrary")),
        name="ada_mod",
    )(c_all, w_ada, b_ada.reshape(depth, 1, d6))


def _window_sums(ext):
    d = ext.shape[1]
    grp = d // len(POOL_WINDOWS)
    s2 = ext + pltpu.roll(ext, 1, 0)
    r4 = s2[:, grp:]
    s4 = r4 + pltpu.roll(r4, 2, 0)
    r8 = s4[:, grp:]
    s8 = r8 + pltpu.roll(r8, 4, 0)
    r16 = s8[:, grp:]
    s16 = r16 + pltpu.roll(r16, 8, 0)
    return jnp.concatenate(
        [s2[HIST_ROWS:, :grp], s4[HIST_ROWS:, :grp], s8[HIST_ROWS:, :grp],
         s16[HIST_ROWS:, :]], axis=1)


def _inv_count(t_idx, rows, window):
    pos = lax.broadcasted_iota(jnp.int32, (HIST_ROWS, V7X_LANES), 0)
    head = 1.0 / jnp.minimum(pos + 1, window).astype(F32)
    head = jnp.where(t_idx == 0, head, 1.0 / window)
    tail = jnp.full((rows - HIST_ROWS, V7X_LANES), 1.0 / window, F32)
    return jnp.concatenate([head, tail], axis=0)


def _pool_layer_kernel(final, x_ref, mod_ref, nmix_ref, nffn_ref, nfin_ref,
                       wgrp_ref, pscale_ref, wout_ref, wg_ref, wu_ref, wd_ref,
                       o_ref, newpool_ref, hist_ref, a_ref):
    t = pl.program_id(1)
    tm, d = x_ref.shape
    n_grp = len(POOL_WINDOWS)
    grp = d // n_grp

    @pl.when(t == 0)
    def _():
        hist_ref[...] = jnp.zeros_like(hist_ref)

    x = x_ref[...]
    mod = mod_ref[...]
    sh1, sc1, g1, sh2, sc2, g2 = [mod[k:k + 1, :] for k in range(N_MOD)]

    h = _rms_mod(x, nmix_ref[...], sc1, sh1)
    ext = jnp.concatenate([hist_ref[...], h], axis=0)
    sums = _window_sums(ext)
    tail = h[tm - HIST_ROWS:, :]
    hist_ref[...] = tail
    newpool_ref[...] = tail

    mixed = []
    for g, w in enumerate(POOL_WINDOWS):
        inv = _inv_count(t, tm, w)
        inv = jnp.concatenate([inv] * (grp // V7X_LANES), axis=1)
        sl = slice(g * grp, (g + 1) * grp)
        pooled = sums[:, sl] * inv - h[:, sl]
        mixed.append(jnp.dot(pooled.astype(BF16), wgrp_ref[g],
                             preferred_element_type=F32))
    mixed = jnp.concatenate(mixed, axis=1) * pscale_ref[...]
    out = jnp.dot(mixed.astype(BF16), wout_ref[...], preferred_element_type=F32)
    x1 = x + g1 * out

    h2b = _rms_mod(x1, nffn_ref[...], sc2, sh2).astype(BF16)
    ff = _ffn(h2b, wg_ref, wu_ref, wd_ref, a_ref)
    y = x1 + g2 * ff
    if final:
        y = _rms(y, nfin_ref[...])
    o_ref[...] = y


def _layer_block(stacked, layer):
    tail = stacked.shape[1:]
    origin = (0,) * len(tail)
    return pl.BlockSpec((None,) + tail, lambda *_: (layer,) + origin,
                        pipeline_mode=pl.Buffered(1))


def _pool_layer_call(x, mods, layer, mixer_layer, final, nmix, nffn, nfin,
                     wgrp, pscale, wout, wg, wu, wd):
    bsz, seq, d = x.shape
    tm = TOKEN_TILE
    d_ff = wg.shape[2]
    return pl.pallas_call(
        functools.partial(_pool_layer_kernel, final),
        grid=(bsz, seq // tm),
        in_specs=[
            pl.BlockSpec((None, tm, d), lambda b, t: (b, t, 0)),
            pl.BlockSpec((None, None, N_MOD, d), lambda b, t: (layer, b, 0, 0)),
            _layer_block(nmix, layer), _layer_block(nffn, layer), _layer_block(nfin, 0),
            _layer_block(wgrp, mixer_layer), _layer_block(pscale, mixer_layer),
            _layer_block(wout, mixer_layer),
            _layer_block(wg, layer), _layer_block(wu, layer), _layer_block(wd, layer),
        ],
        out_specs=[
            pl.BlockSpec((None, tm, d), lambda b, t: (b, t, 0)),
            pl.BlockSpec((None, HIST_ROWS, d), lambda b, t: (b, 0, 0)),
        ],
        out_shape=[
            jax.ShapeDtypeStruct((bsz, seq, d), F32),
            jax.ShapeDtypeStruct((bsz, HIST_ROWS, d), F32),
        ],
        scratch_shapes=[
            pltpu.VMEM((HIST_ROWS, d), F32),
            pltpu.VMEM((tm, d_ff), BF16),
        ],
        compiler_params=pltpu.CompilerParams(
            dimension_semantics=("arbitrary", "arbitrary"),
            vmem_limit_bytes=V7X_VMEM_LIMIT_BYTES),
        name="prompt_pool_layer",
    )(x, mods, nmix, nffn, nfin, wgrp, pscale, wout, wg, wu, wd)


def _layernorm(v, gain, bias):
    mu = jnp.mean(v, axis=-1, keepdims=True)
    vc = v - mu
    var = jnp.mean(vc * vc, axis=-1, keepdims=True)
    return vc * lax.rsqrt(var + EPS) * gain + bias


def _gmlp_layer_kernel(final, x_ref, mod_ref, nmix_ref, nffn_ref, nfin_ref,
                       win_ref, lng_ref, lnb_ref, ws_ref, bsp_ref, wout_ref,
                       wg_ref, wu_ref, wd_ref, o_ref, um_ref, a_ref):
    tm, d = x_ref.shape
    gc = d // GMLP_GROUPS
    n_chunks = tm // CHUNK

    x = x_ref[...]
    mod = mod_ref[...]
    sh1, sc1, g1, sh2, sc2, g2 = [mod[k:k + 1, :] for k in range(N_MOD)]

    hb = _rms_mod(x, nmix_ref[...], sc1, sh1).astype(BF16)
    uv = jnp.dot(hb, win_ref[...], preferred_element_type=F32)
    u = uv[:, :d]
    vb = _layernorm(uv[:, d:], lng_ref[...], lnb_ref[...]).astype(BF16)

    row = lax.broadcasted_iota(jnp.int32, (CHUNK, CHUNK), 0)
    col = lax.broadcasted_iota(jnp.int32, (CHUNK, CHUNK), 1)
    causal = row >= col
    bsp = bsp_ref[...]
    for g in range(GMLP_GROUPS):
        lanes = slice(g * gc, (g + 1) * gc)
        ws = jnp.where(causal, ws_ref[g], 0.0).astype(BF16)
        rhs = jnp.concatenate(
            [vb[c * CHUNK:(c + 1) * CHUNK, lanes] for c in range(n_chunks)], axis=1)
        mix = jnp.dot(ws, rhs, preferred_element_type=F32)
        for c in range(n_chunks):
            rows = slice(c * CHUNK, (c + 1) * CHUNK)
            m = mix[:, c * gc:(c + 1) * gc] + bsp[:, lanes]
            um_ref[rows, lanes] = (u[rows, lanes] * m).astype(BF16)
    out = jnp.dot(um_ref[...], wout_ref[...], preferred_element_type=F32)
    x1 = x + g1 * out

    h2b = _rms_mod(x1, nffn_ref[...], sc2, sh2).astype(BF16)
    ff = _ffn(h2b, wg_ref, wu_ref, wd_ref, a_ref)
    y = x1 + g2 * ff
    if final:
        y = _rms(y, nfin_ref[...])
    o_ref[...] = y


def _gmlp_layer_call(x, mods, layer, mixer_layer, final, nmix, nffn, nfin,
                     win, lng, lnb, ws, bsp, wout, wg, wu, wd):
    bsz, seq, d = x.shape
    tm = TOKEN_TILE
    d_ff = wg.shape[2]
    return pl.pallas_call(
        functools.partial(_gmlp_layer_kernel, final),
        grid=(bsz, seq // tm),
        in_specs=[
            pl.BlockSpec((None, tm, d), lambda b, t: (b, t, 0)),
            pl.BlockSpec((None, None, N_MOD, d), lambda b, t: (layer, b, 0, 0)),
            _layer_block(nmix, layer), _layer_block(nffn, layer), _layer_block(nfin, 0),
            _layer_block(win, mixer_layer), _layer_block(lng, mixer_layer),
            _layer_block(lnb, mixer_layer), _layer_block(ws, mixer_layer),
            _layer_block(bsp, mixer_layer), _layer_block(wout, mixer_layer),
            _layer_block(wg, layer), _layer_block(wu, layer), _layer_block(wd, layer),
        ],
        out_specs=pl.BlockSpec((None, tm, d), lambda b, t: (b, t, 0)),
        out_shape=jax.ShapeDtypeStruct((bsz, seq, d), F32),
        scratch_shapes=[
            pltpu.VMEM((tm, d), BF16),
            pltpu.VMEM((tm, d_ff), BF16),
        ],
        compiler_params=pltpu.CompilerParams(
            dimension_semantics=("arbitrary", "arbitrary"),
            vmem_limit_bytes=V7X_VMEM_LIMIT_BYTES),
        name="prompt_gmlp_layer",
    )(x, mods, nmix, nffn, nfin, win, lng, lnb, ws, bsp, wout, wg, wu, wd)


def _sample_tail(final, c, n_c, mod_ref, nfin_ref, wg_ref, wu_ref, wd_ref,
                 o_ref, x1_ref, h2_ref, acc_ref):
    h2b = h2_ref[...]
    g = jnp.dot(h2b, wg_ref[...], preferred_element_type=F32)
    u = jnp.dot(h2b, wu_ref[...], preferred_element_type=F32)
    a = _silu_mul(g, u).astype(BF16)
    acc_ref[...] += jnp.dot(a, wd_ref[...], preferred_element_type=F32)

    @pl.when(c == n_c - 1)
    def _():
        y = x1_ref[...] + mod_ref[5] * acc_ref[...]
        if final:
            y = _rms(y, nfin_ref[...])
        o_ref[...] = y


def _sample_pool_kernel(final, x_ref, mod_ref, nmix_ref, nffn_ref, nfin_ref,
                        hist_ref, wgrp_ref, pscale_ref, wout_ref,
                        wg_ref, wu_ref, wd_ref,
                        o_ref, hnew_ref, x1_ref, h2_ref, acc_ref):
    c = pl.program_id(0)
    n_c = pl.num_programs(0)

    @pl.when(c == 0)
    def _():
        x = x_ref[...]
        d = x.shape[1]
        grp = d // len(POOL_WINDOWS)
        h = _rms_mod(x, nmix_ref[...], mod_ref[1], mod_ref[0])
        hnew_ref[...] = h
        mixed = []
        for g, w in enumerate(POOL_WINDOWS):
            sl = slice(g * grp, (g + 1) * grp)
            s = h[:, sl]
            for k in range(POOL_BUF - (w - 1), POOL_BUF):
                s = s + hist_ref[k, :, sl]
            pooled = s * (1.0 / w) - h[:, sl]
            mixed.append(jnp.dot(pooled.astype(BF16), wgrp_ref[g],
                                 preferred_element_type=F32))
        mixed = jnp.concatenate(mixed, axis=1) * pscale_ref[...]
        out = jnp.dot(mixed.astype(BF16), wout_ref[...], preferred_element_type=F32)
        x1 = x + mod_ref[2] * out
        x1_ref[...] = x1
        h2_ref[...] = _rms_mod(x1, nffn_ref[...], mod_ref[4], mod_ref[3]).astype(BF16)
        acc_ref[...] = jnp.zeros_like(acc_ref)

    _sample_tail(final, c, n_c, mod_ref, nfin_ref, wg_ref, wu_ref, wd_ref,
                 o_ref, x1_ref, h2_ref, acc_ref)


def _sample_gmlp_kernel(final, x_ref, mod_ref, nmix_ref, nffn_ref, nfin_ref,
                        win_ref, lng_ref, lnb_ref, ws0_ref, bs0_ref, wout_ref,
                        wg_ref, wu_ref, wd_ref,
                        o_ref, vnew_ref, x1_ref, h2_ref, acc_ref):
    c = pl.program_id(0)
    n_c = pl.num_programs(0)

    @pl.when(c == 0)
    def _():
        x = x_ref[...]
        d = x.shape[1]
        hb = _rms_mod(x, nmix_ref[...], mod_ref[1], mod_ref[0]).astype(BF16)
        uv = jnp.dot(hb, win_ref[...], preferred_element_type=F32)
        v = _layernorm(uv[:, d:], lng_ref[...], lnb_ref[...])
        vnew_ref[...] = v
        um = uv[:, :d] * (ws0_ref[...] * v + bs0_ref[...])
        out = jnp.dot(um.astype(BF16), wout_ref[...], preferred_element_type=F32)
        x1 = x + mod_ref[2] * out
        x1_ref[...] = x1
        h2_ref[...] = _rms_mod(x1, nffn_ref[...], mod_ref[4], mod_ref[3]).astype(BF16)
        acc_ref[...] = jnp.zeros_like(acc_ref)

    _sample_tail(final, c, n_c, mod_ref, nfin_ref, wg_ref, wu_ref, wd_ref,
                 o_ref, x1_ref, h2_ref, acc_ref)


def _fixed(shape):
    nd = len(shape)
    return pl.BlockSpec(shape, lambda c: (0,) * nd)


def _sample_call(kernel, name, x, mods, layer, mixer_args, wg, wu, wd, aux_shape):
    rows, d = x.shape
    d_ff = wg.shape[2]
    n_c = d_ff // FFN_CHUNK
    stacked = [a for a, _ in mixer_args]
    mods_spec = pl.BlockSpec((None, N_MOD, rows, d), lambda c: (layer, 0, 0, 0),
                             pipeline_mode=pl.Buffered(1))
    return pl.pallas_call(
        kernel,
        grid=(n_c,),
        in_specs=[_fixed(x.shape), mods_spec]
        + [_layer_block(a, k) for a, k in mixer_args] + [
            pl.BlockSpec((None, d, FFN_CHUNK), lambda c: (layer, 0, c)),
            pl.BlockSpec((None, d, FFN_CHUNK), lambda c: (layer, 0, c)),
            pl.BlockSpec((None, FFN_CHUNK, d), lambda c: (layer, c, 0)),
        ],
        out_specs=[_fixed((rows, d)), _fixed(aux_shape)],
        out_shape=[jax.ShapeDtypeStruct((rows, d), F32),
                   jax.ShapeDtypeStruct(aux_shape, F32)],
        scratch_shapes=[
            pltpu.VMEM((rows, d), F32),
            pltpu.VMEM((rows, d), BF16),
            pltpu.VMEM((rows, d), F32),
        ],
        compiler_params=pltpu.CompilerParams(
            dimension_semantics=("arbitrary",),
            vmem_limit_bytes=V7X_VMEM_LIMIT_BYTES),
        name=name,
    )(x, mods, *stacked, wg, wu, wd)


def kernel(x_prompt, x_sample, state_pool, c_prompt, c_sample, w_ada, b_ada, norm_mix, norm_ffn, norm_final, pool_w_grp, pool_scale, pool_w_out, gmlp_w_in, gmlp_ln_g, gmlp_ln_b, gmlp_w_s, gmlp_b_s, gmlp_w_out, ffn_w_gate, ffn_w_up, ffn_w_down):
    depth = w_ada.shape[0]
    bsz, seq, d = x_prompt.shape
    n_s = x_sample.shape[0]
    gc = d // GMLP_GROUPS
    assert x_sample.shape[1] == 1 and seq % TOKEN_TILE == 0 and TOKEN_TILE % CHUNK == 0
    assert state_pool.shape[2] == POOL_BUF and d % (len(POOL_WINDOWS) * V7X_LANES) == 0

    pad = (-(n_s + bsz)) % ADA_ROWS_ALIGN
    c_all = jnp.concatenate([c_sample, c_prompt, jnp.zeros((pad, d), F32)], axis=0)
    mods = _ada_call(c_all, w_ada, b_ada)
    mod_p = jnp.transpose(mods[:, :, n_s:n_s + bsz], (0, 2, 1, 3))

    bf = lambda w: w.astype(BF16)
    ffn_w = (bf(ffn_w_gate), bf(ffn_w_up), bf(ffn_w_down))
    pool_wgrp, pool_wout = bf(pool_w_grp), bf(pool_w_out)
    gmlp_win, gmlp_wout = bf(gmlp_w_in), bf(gmlp_w_out)
    hist_t = jnp.transpose(state_pool, (0, 2, 1, 3))
    rows3 = lambda v: v.reshape(-1, 1, d)
    norms = (rows3(norm_mix), rows3(norm_ffn), rows3(norm_final))
    pscale, lng, lnb = rows3(pool_scale), rows3(gmlp_ln_g), rows3(gmlp_ln_b)
    bsp = jnp.repeat(jnp.swapaxes(gmlp_b_s, 1, 2), gc, axis=2)
    ws0 = rows3(jnp.repeat(gmlp_w_s[:, :, 0, 0], gc, axis=1))
    bs0 = rows3(jnp.repeat(gmlp_b_s[:, :, 0], gc, axis=1))

    xp = x_prompt
    xs = x_sample.reshape(n_s, d)
    new_pool_p, new_pool_s, new_v_s = [], [], []
    for i in range(depth):
        final = i == depth - 1
        j = i // 2
        s_norms = ((norms[0], i), (norms[1], i), (norms[2], 0))
        if i % 2 == 0:
            xp, tail = _pool_layer_call(
                xp, mod_p, i, j, final, *norms, pool_wgrp, pscale, pool_wout, *ffn_w)
            new_pool_p.append(tail[:, HIST_ROWS - POOL_BUF:])
            xs, h_s = _sample_call(
                functools.partial(_sample_pool_kernel, final), "sample_pool_layer", xs,
                mods, i,
                (*s_norms, (hist_t, j), (pool_wgrp, j), (pscale, j), (pool_wout, j)),
                *ffn_w, (n_s, d))
            new_pool_s.append(
                jnp.concatenate([state_pool[j][:, 1:], h_s[:, None, :]], axis=1))
        else:
            xp = _gmlp_layer_call(
                xp, mod_p, i, j, final, *norms, gmlp_win, lng, lnb, gmlp_w_s, bsp,
                gmlp_wout, *ffn_w)
            xs, v_s = _sample_call(
                functools.partial(_sample_gmlp_kernel, final), "sample_gmlp_layer", xs,
                mods, i,
                (*s_norms, (gmlp_win, j), (lng, j), (lnb, j), (ws0, j), (bs0, j),
                 (gmlp_wout, j)),
                *ffn_w, (n_s, d))
            new_v_s.append(v_s[:, None, :])

    return (xp, xs.reshape(n_s, 1, d), jnp.stack(new_pool_p), jnp.stack(new_pool_s),
            jnp.stack(new_v_s))
```
